```python
import math
import jax
import jax.numpy as jnp
from jax import lax
import numpy as np

D_MODEL = 2048
BATCH = 2
SEQ = 4096
DEPTH = 2
DEC_BATCH = 16
DEC_SEQ = 2048
PAST_LEN = 128

GRID_W = 64
HEAD_DIM = 128
BRANCH_WIDTH = 1024
N_BRANCHES = 4
EPS = 1e-6
ATTN_HEADS = 8
ATTN_KV_HEADS = 2
AXIAL_THETA = 10000.0
Q_BLOCK = 128
SSM_INNER = 1024
SSM_HEAD_DIM = 64
SSM_HEADS = SSM_INNER // SSM_HEAD_DIM
SSM_GROUPS = 2
SSM_STATE = 128
SSM_CONV = 5
SSM_CHUNK = 128
SSM_CONV_DIM = SSM_INNER + 2 * SSM_GROUPS * SSM_STATE
GLA_HEADS = 4
GLA_KEY_DIM = 128
GLA_VAL_DIM = 256
GLA_GATE_RANK = 16
GLA_GATE_NORMALIZER = 16.0
GLA_CHUNK = 64
DIL_PAIRS = ((128, 1), (512, 4), (2048, 16))
DIL_HEADS = 8
ROPE_THETA = 500000.0
ROPE_DIMS = HEAD_DIM // 4
N_EXPERTS = 16
EXPERT_FF = 4096
EC_CAPACITY = 2

SPLIT_SIZES = (
    ATTN_HEADS * HEAD_DIM, ATTN_KV_HEADS * HEAD_DIM, ATTN_KV_HEADS * HEAD_DIM,
    SSM_INNER, SSM_CONV_DIM, 2 * SSM_HEADS,
    GLA_HEADS * GLA_KEY_DIM, GLA_HEADS * GLA_KEY_DIM, GLA_HEADS * GLA_VAL_DIM,
    GLA_HEADS * GLA_VAL_DIM, 2 * GLA_GATE_RANK,
    len(DIL_PAIRS) * DIL_HEADS * HEAD_DIM, len(DIL_PAIRS) * DIL_HEADS * HEAD_DIM,
    len(DIL_PAIRS) * DIL_HEADS * HEAD_DIM,
    N_BRANCHES * D_MODEL,
)
IN_WIDTH = sum(SPLIT_SIZES)

kernel_name = 'hybrid_bidir_encoder'


def _rms_norm(x, g):
    xf = x.astype(jnp.float32)
    y = xf * lax.rsqrt(jnp.mean(xf * xf, axis=-1, keepdims=True) + EPS)
    return (y * g.astype(jnp.float32)).astype(x.dtype)


def _flip(t):
    return jnp.flip(t, axis=1)


def _rope_tables(pos, dims, theta):
    freqs = theta ** (-jnp.arange(0, dims, 2, dtype=jnp.float32) / dims)
    ang = pos[:, None] * freqs[None, :]
    return jnp.cos(ang), jnp.sin(ang)


def _rotary(t, cos, sin):
    m = t.shape[-1] // 2
    c = cos[None, :, None, :].astype(t.dtype)
    s = sin[None, :, None, :].astype(t.dtype)
    t1, t2 = t[..., :m], t[..., m:]
    return jnp.concatenate([t1 * c - t2 * s, t2 * c + t1 * s], axis=-1)


def _blocked_gqa(q, k, v):
    b, L, hq, hd = q.shape
    hkv = k.shape[2]
    rep = hq // hkv
    nb = L // Q_BLOCK
    qb = jnp.moveaxis(q.reshape(b, nb, Q_BLOCK, hkv, rep, hd), 1, 0)

    def attend(qi):
        s = jnp.einsum('bqgrd,bkgd->bgrqk', qi, k).astype(jnp.float32)
        p = jax.nn.softmax(s, axis=-1).astype(v.dtype)
        return jnp.einsum('bgrqk,bkgd->bqgrd', p, v)

    o = lax.map(attend, qb)
    return jnp.moveaxis(o, 0, 1).reshape(b, L, hq * hd)


def _axial_gqa_branch(aq, ak, av, q_norm, k_norm):
    b, L, _ = aq.shape
    rows = L // GRID_W
    row = jnp.repeat(jnp.arange(rows, dtype=jnp.float32), GRID_W)
    col = jnp.arange(L, dtype=jnp.float32) - row * GRID_W
    half = HEAD_DIM // 2
    cr, sr = _rope_tables(row, half, AXIAL_THETA)
    cc, sc = _rope_tables(col, half, AXIAL_THETA)

    def axial(t):
        return jnp.concatenate([_rotary(t[..., :half], cr, sr), _rotary(t[..., half:], cc, sc)], axis=-1)

    q = axial(_rms_norm(aq.reshape(b, L, ATTN_HEADS, HEAD_DIM), q_norm)) * HEAD_DIM ** -0.5
    k = axial(_rms_norm(ak.reshape(b, L, ATTN_KV_HEADS, HEAD_DIM), k_norm))
    v = av.reshape(b, L, ATTN_KV_HEADS, HEAD_DIM)
    return _blocked_gqa(q, k, v).astype(aq.dtype)


def _centred_depthwise_conv(x, w):
    kw = w.shape[0]
    return lax.conv_general_dilated(
        x, w.astype(x.dtype)[:, None, :], window_strides=(1,), padding=[(kw // 2, kw // 2)],
        dimension_numbers=('NWC', 'WIO', 'NWC'), feature_group_count=x.shape[-1])


def _ssd_scan(x, a, bm, cm):
    b, L, g, hg, p = x.shape
    n = bm.shape[-1]
    nc, l = L // SSM_CHUNK, SSM_CHUNK
    x = x.reshape(b, nc, l, g, hg, p)
    a = a.reshape(b, nc, l, g, hg)
    bm = bm.reshape(b, nc, l, g, n)
    cm = cm.reshape(b, nc, l, g, n)
    acs = jnp.cumsum(a, axis=2)
    lower = jnp.tril(jnp.ones((l, l), dtype=bool))
    seg = acs[:, :, :, None] - acs[:, :, None, :]
    decay = jnp.exp(jnp.where(lower[:, :, None, None], seg, -jnp.inf))
    scores = jnp.einsum('bclgn,bcsgn->bclsg', cm, bm)
    y = jnp.einsum('bclsgh,bcsghp->bclghp', scores[..., None] * decay, x)
    a_last = acs[:, :, -1]
    states = jnp.einsum('bclgn,bclghp->bcghpn', bm, x * jnp.exp(a_last[:, :, None] - acs)[..., None])

    def step(s, inp):
        dec, st = inp
        return jnp.exp(dec)[..., None, None] * s + st, s

    _, s_prev = lax.scan(step, jnp.zeros((b, g, hg, p, n), jnp.float32),
                         (jnp.moveaxis(a_last, 1, 0), jnp.moveaxis(states, 1, 0)))
    s_prev = jnp.moveaxis(s_prev, 0, 1)
    y = y + jnp.einsum('bclgn,bcghpn->bclghp', cm, s_prev) * jnp.exp(acs)[..., None]
    return y.reshape(b, L, g, hg, p)


def _ssd_branch(sz, sxbc, sdt, conv_w, conv_b, a_log, dt_bias, d_skip, norm_g):
    b, L, _ = sxbc.shape
    f32 = jnp.float32
    hg = SSM_HEADS // SSM_GROUPS
    xbc = jax.nn.silu(_centred_depthwise_conv(sxbc, conv_w).astype(f32) + conv_b.astype(f32))
    xs, bm, cm = jnp.split(xbc, [SSM_INNER, SSM_INNER + SSM_GROUPS * SSM_STATE], axis=-1)
    xs = xs.reshape(b, L, SSM_GROUPS, hg, SSM_HEAD_DIM)
    bm = bm.reshape(b, L, SSM_GROUPS, SSM_STATE)
    cm = cm.reshape(b, L, SSM_GROUPS, SSM_STATE)
    dt = jax.nn.softplus(sdt.astype(f32).reshape(b, L, 2, SSM_HEADS) + dt_bias.astype(f32))
    dt = dt.reshape(b, L, 2, SSM_GROUPS, hg)
    a = -jnp.exp(a_log.astype(f32)).reshape(2, SSM_GROUPS, hg)
    y_f = _ssd_scan(xs * dt[:, :, 0, ..., None], dt[:, :, 0] * a[0], bm, cm)
    y_b = _flip(_ssd_scan(_flip(xs * dt[:, :, 1, ..., None]), _flip(dt[:, :, 1] * a[1]), _flip(bm), _flip(cm)))
    y = y_f + y_b + d_skip.astype(f32).reshape(SSM_GROUPS, hg, 1) * xs
    y = y.reshape(b, L, SSM_INNER) * jax.nn.silu(sz.astype(f32))
    y = _rms_norm(y.reshape(b, L, SSM_GROUPS, SSM_INNER // SSM_GROUPS), norm_g.reshape(SSM_GROUPS, -1))
    return y.reshape(b, L, SSM_INNER).astype(sz.dtype)


def _gla_scan(q, k, v, g):
    b, L, h, dk = q.shape
    dv = v.shape[-1]
    nc, l = L // GLA_CHUNK, GLA_CHUNK
    q = q.reshape(b, nc, l, h, dk)
    k = k.reshape(b, nc, l, h, dk)
    g = g.reshape(b, nc, l, h, dk)
    v = v.reshape(b, nc, l, h, dv)
    gc = jnp.cumsum(g, axis=2)
    g_last = gc[:, :, -1]
    q_e = q * jnp.exp(gc)
    k_e = k * jnp.exp(-gc)
    k_s = k * jnp.exp(g_last[:, :, None] - gc)
    lower = jnp.tril(jnp.ones((l, l), dtype=bool))
    att = jnp.where(lower, jnp.einsum('bclhk,bcshk->bchls', q_e, k_e), 0.0)
    o = jnp.einsum('bchls,bcshv->bclhv', att, v)
    chunk_states = jnp.einsum('bclhk,bclhv->bchkv', k_s, v)

    def step(s, inp):
        dec, st = inp
        return jnp.exp(dec)[..., None] * s + st, s

    _, s_prev = lax.scan(step, jnp.zeros((b, h, dk, dv), jnp.float32),
                         (jnp.moveaxis(g_last, 1, 0), jnp.moveaxis(chunk_states, 1, 0)))
    o = o + jnp.einsum('bclhk,bchkv->bclhv', q_e, jnp.moveaxis(s_prev, 0, 1))
    return o.reshape(b, L, h, dv)


def _gla_branch(gq, gk, gv, gr, glr, gate_w, gate_b, norm_g):
    b, L, _ = gq.shape
    f32 = jnp.float32
    q = gq.reshape(b, L, GLA_HEADS, GLA_KEY_DIM).astype(f32) * GLA_KEY_DIM ** -0.5
    k = gk.reshape(b, L, GLA_HEADS, GLA_KEY_DIM).astype(f32)
    v = gv.reshape(b, L, GLA_HEADS, GLA_VAL_DIM).astype(f32)
    lr = glr.reshape(b, L, 2, GLA_GATE_RANK).astype(f32)
    logdecay = jax.nn.log_sigmoid(jnp.einsum('bldr,drk->bldk', lr, gate_w.astype(f32)) + gate_b.astype(f32)) / GLA_GATE_NORMALIZER
    logdecay = logdecay.reshape(b, L, 2, GLA_HEADS, GLA_KEY_DIM)
    o_f = _gla_scan(q, k, v, logdecay[:, :, 0])
    o_b = _flip(_gla_scan(_flip(q), _flip(k), _flip(v), _flip(logdecay[:, :, 1])))
    o = _rms_norm(o_f + o_b, norm_g) * jax.nn.silu(gr.reshape(b, L, GLA_HEADS, GLA_VAL_DIM).astype(f32))
    return o.reshape(b, L, GLA_HEADS * GLA_VAL_DIM).astype(gq.dtype)


def _banded_attention(q, k, v, half):
    nbat, h, n, hd = q.shape
    nblk = -(-n // half)
    npad = nblk * half
    qb = jnp.pad(q, ((0, 0), (0, 0), (0, npad - n), (0, 0))).reshape(nbat, h, nblk, half, hd)
    pad_kv = ((0, 0), (0, 0), (half, npad - n + half), (0, 0))

    def bands(t):
        tp = jnp.pad(t, pad_kv).reshape(nbat, h, nblk + 2, half, hd)
        return jnp.concatenate([tp[:, :, :nblk], tp[:, :, 1:nblk + 1], tp[:, :, 2:]], axis=3)

    kb, vb = bands(k), bands(v)
    s = jnp.einsum('bhiqd,bhikd->bhiqk', qb, kb).astype(jnp.float32)
    start = jnp.arange(nblk)[:, None] * half
    qpos = start + jnp.arange(half)[None, :]
    kpos = start - half + jnp.arange(3 * half)[None, :]
    valid = ((jnp.abs(qpos[:, :, None] - kpos[:, None, :]) <= half)
             & (kpos >= 0)[:, None, :] & (kpos < n)[:, None, :])
    s = jnp.where(valid, s, -jnp.inf)
    lse = jax.nn.logsumexp(s, axis=-1)
    p = jnp.exp(s - lse[..., None]).astype(v.dtype)
    o = jnp.einsum('bhiqk,bhikd->bhiqd', p, vb)
    return o.reshape(nbat, h, npad, hd)[:, :, :n], lse.reshape(nbat, h, npad)[:, :, :n]


def _to_residues(t, dil):
    b, L, h, hd = t.shape
    return t.reshape(b, L // dil, dil, h, hd).transpose(0, 2, 3, 1, 4).reshape(b * dil, h, L // dil, hd)


def _dilated_branch(dq, dk, dv, q_norm, k_norm):
    b, L, _ = dq.shape
    shape = (b, L, len(DIL_PAIRS) * DIL_HEADS, HEAD_DIM)
    cos, sin = _rope_tables(jnp.arange(L, dtype=jnp.float32), ROPE_DIMS, ROPE_THETA)

    def partial_rope(t):
        return jnp.concatenate([_rotary(t[..., :ROPE_DIMS], cos, sin), t[..., ROPE_DIMS:]], axis=-1)

    q = partial_rope(_rms_norm(dq.reshape(shape), q_norm)) * HEAD_DIM ** -0.5
    k = partial_rope(_rms_norm(dk.reshape(shape), k_norm))
    v = dv.reshape(shape)
    outs, lses = [], []
    for gi, (window, dil) in enumerate(DIL_PAIRS):
        hs = slice(gi * DIL_HEADS, (gi + 1) * DIL_HEADS)
        o, lse = _banded_attention(_to_residues(q[:, :, hs], dil), _to_residues(k[:, :, hs], dil),
                                   _to_residues(v[:, :, hs], dil), window // (2 * dil))
        o = o.reshape(b, dil, DIL_HEADS, L // dil, HEAD_DIM).transpose(0, 3, 1, 2, 4).reshape(b, L, DIL_HEADS, HEAD_DIM)
        lse = lse.reshape(b, dil, DIL_HEADS, L // dil).transpose(0, 3, 1, 2).reshape(b, L, DIL_HEADS)
        outs.append(o)
        lses.append(lse)
    wts = jax.nn.softmax(jnp.stack(lses, axis=0), axis=0)
    o = wts[0][..., None].astype(outs[0].dtype) * outs[0]
    for gi in range(1, len(DIL_PAIRS)):
        o = o + wts[gi][..., None].astype(outs[gi].dtype) * outs[gi]
    return o.reshape(b, L, DIL_HEADS * HEAD_DIM).astype(dq.dtype)


def _token_mixer(h, w_in, attn_q_norm, attn_k_norm, ssm_conv_w, ssm_conv_b, ssm_a_log, ssm_dt_bias,
                 ssm_d, ssm_norm, gla_gate_w, gla_gate_b, gla_norm, dil_q_norm, dil_k_norm, w_branch, w_out):
    b, L, _ = h.shape
    u = h @ w_in
    points, acc = [], 0
    for size in SPLIT_SIZES[:-1]:
        acc += size
        points.append(acc)
    (aq, ak, av, sz, sxbc, sdt, gq, gk, gv, gr, glr, dq, dk, dv, gate_logits) = jnp.split(u, points, axis=-1)
    branches = (
        _axial_gqa_branch(aq, ak, av, attn_q_norm, attn_k_norm),
        _ssd_branch(sz, sxbc, sdt, ssm_conv_w, ssm_conv_b, ssm_a_log, ssm_dt_bias, ssm_d, ssm_norm),
        _gla_branch(gq, gk, gv, gr, glr, gla_gate_w, gla_gate_b, gla_norm),
        _dilated_branch(dq, dk, dv, dil_q_norm, dil_k_norm),
    )
    gates = jax.nn.sigmoid(gate_logits.reshape(b, L, N_BRANCHES, D_MODEL))
    merged = gates[:, :, 0] * (branches[0] @ w_branch[0])
    for i in range(1, N_BRANCHES):
        merged = merged + gates[:, :, i] * (branches[i] @ w_branch[i])
    return merged @ w_out


def _expert_choice_ffn(h, w_router, w_gate, w_up, w_down):
    b, L, d = h.shape
    n = b * L
    cap = EC_CAPACITY * n // N_EXPERTS
    t = h.reshape(n, d)
    aff = jax.nn.softmax((t @ w_router).astype(jnp.float32), axis=-1)
    gate, idx = lax.top_k(aff.T, cap)
    xe = jnp.take(t, idx, axis=0)
    hid = jax.nn.silu(jnp.einsum('ecd,edf->ecf', xe, w_gate)) * jnp.einsum('ecd,edf->ecf', xe, w_up)
    ye = jnp.einsum('ecf,efd->ecd', hid, w_down) * gate[..., None].astype(h.dtype)
    out = jnp.zeros_like(t).at[idx.reshape(-1)].add(ye.reshape(-1, d))
    return out.reshape(b, L, d)


def setup_inputs(seed: int = 0) -> dict:
    key = jax.random.key(seed)
    ks = jax.random.split(key, 24)
    f32 = jnp.float32

    def nrm(k, shape, scale):
        return jax.random.normal(k, shape, f32) * scale

    def gain(k, shape):
        return 1.0 + 0.02 * jax.random.normal(k, shape, f32)

    dt0 = jnp.exp(jax.random.uniform(ks[9], (DEPTH, 2, SSM_HEADS), f32, math.log(1e-3), math.log(1e-1)))
    return {
        'x_prompt': nrm(ks[0], (BATCH, SEQ, D_MODEL), 1.0),
        'x_sample': nrm(ks[1], (DEC_BATCH, DEC_SEQ, D_MODEL), 1.0),
        'norm_mix': gain(ks[2], (DEPTH, D_MODEL)),
        'w_in': nrm(ks[3], (DEPTH, D_MODEL, IN_WIDTH), D_MODEL ** -0.5),
        'attn_q_norm': gain(ks[4], (DEPTH, HEAD_DIM)),
        'attn_k_norm': gain(ks[5], (DEPTH, HEAD_DIM)),
        'ssm_conv_w': nrm(ks[6], (DEPTH, SSM_CONV, SSM_CONV_DIM), SSM_CONV ** -0.5),
        'ssm_conv_b': nrm(ks[7], (DEPTH, SSM_CONV_DIM), 0.01),
        'ssm_a_log': jnp.log(jax.random.uniform(ks[8], (DEPTH, 2, SSM_HEADS), f32, 1.0, 16.0)),
        'ssm_dt_bias': dt0 + jnp.log(-jnp.expm1(-dt0)),
        'ssm_d': 1.0 + 0.1 * jax.random.normal(ks[10], (DEPTH, SSM_HEADS), f32),
        'ssm_norm': gain(ks[11], (DEPTH, SSM_INNER)),
        'gla_gate_w': nrm(ks[12], (DEPTH, 2, GLA_GATE_RANK, GLA_HEADS * GLA_KEY_DIM), GLA_GATE_RANK ** -0.5),
        'gla_gate_b': nrm(ks[13], (DEPTH, 2, GLA_HEADS * GLA_KEY_DIM), 0.01),
        'gla_norm': gain(ks[14], (DEPTH, GLA_VAL_DIM)),
        'dil_q_norm': gain(ks[15], (DEPTH, HEAD_DIM)),
        'dil_k_norm': gain(ks[16], (DEPTH, HEAD_DIM)),
        'w_branch': nrm(ks[17], (DEPTH, N_BRANCHES, BRANCH_WIDTH, D_MODEL), BRANCH_WIDTH ** -0.5),
        'w_out': nrm(ks[18], (DEPTH, D_MODEL, D_MODEL), D_MODEL ** -0.5),
        'norm_ffn': gain(ks[19], (DEPTH, D_MODEL)),
        'w_router': nrm(ks[20], (DEPTH, D_MODEL, N_EXPERTS), D_MODEL ** -0.5),
        'w_expert_gate': nrm(ks[21], (DEPTH, N_EXPERTS, D_MODEL, EXPERT_FF), D_MODEL ** -0.5),
        'w_expert_up': nrm(ks[22], (DEPTH, N_EXPERTS, D_MODEL, EXPERT_FF), D_MODEL ** -0.5),
        'w_expert_down': nrm(ks[23], (DEPTH, N_EXPERTS, EXPERT_FF, D_MODEL), EXPERT_FF ** -0.5),
    }


def reference(x_prompt, x_sample, norm_mix, w_in, attn_q_norm, attn_k_norm, ssm_conv_w, ssm_conv_b,
              ssm_a_log, ssm_dt_bias, ssm_d, ssm_norm, gla_gate_w, gla_gate_b, gla_norm, dil_q_norm,
              dil_k_norm, w_branch, w_out, norm_ffn, w_router, w_expert_gate, w_expert_up, w_expert_down):
    def trunk(x):
        for l in range(DEPTH):
            x = x + _token_mixer(_rms_norm(x, norm_mix[l]), w_in[l], attn_q_norm[l], attn_k_norm[l],
                                 ssm_conv_w[l], ssm_conv_b[l], ssm_a_log[l], ssm_dt_bias[l], ssm_d[l],
                                 ssm_norm[l], gla_gate_w[l], gla_gate_b[l], gla_norm[l], dil_q_norm[l],
                                 dil_k_norm[l], w_branch[l], w_out[l])
            x = x + _expert_choice_ffn(_rms_norm(x, norm_ffn[l]), w_router[l], w_expert_gate[l],
                                       w_expert_up[l], w_expert_down[l])
        return x

    y_prompt = trunk(x_prompt)
    y_sample = trunk(x_sample)
    return (y_prompt, y_sample)
```

```python
import functools
import math

import jax
import jax.numpy as jnp
from jax import lax
from jax.experimental import pallas as pl
from jax.experimental.pallas import tpu as pltpu

D_MODEL = 2048
DEPTH = 2
GRID_W = 64
HEAD_DIM = 128
BRANCH_WIDTH = 1024
N_BRANCHES = 4
EPS = 1e-6
ATTN_HEADS = 8
ATTN_KV_HEADS = 2
AXIAL_THETA = 10000.0
Q_BLOCK = 128
SSM_INNER = 1024
SSM_HEAD_DIM = 64
SSM_HEADS = SSM_INNER // SSM_HEAD_DIM
SSM_GROUPS = 2
SSM_STATE = 128
SSM_CONV = 5
SSM_CHUNK = 128
SSM_CONV_DIM = SSM_INNER + 2 * SSM_GROUPS * SSM_STATE
GLA_HEADS = 4
GLA_KEY_DIM = 128
GLA_VAL_DIM = 256
GLA_GATE_RANK = 16
GLA_GATE_NORMALIZER = 16.0
GLA_CHUNK = 64
DIL_PAIRS = ((128, 1), (512, 4), (2048, 16))
DIL_HEADS = 8
ROPE_THETA = 500000.0
ROPE_DIMS = HEAD_DIM // 4
N_EXPERTS = 16
EXPERT_FF = 4096
EC_CAPACITY = 2

SPLIT_SIZES = (
    ATTN_HEADS * HEAD_DIM, ATTN_KV_HEADS * HEAD_DIM, ATTN_KV_HEADS * HEAD_DIM,
    SSM_INNER, SSM_CONV_DIM, 2 * SSM_HEADS,
    GLA_HEADS * GLA_KEY_DIM, GLA_HEADS * GLA_KEY_DIM, GLA_HEADS * GLA_VAL_DIM,
    GLA_HEADS * GLA_VAL_DIM, 2 * GLA_GATE_RANK,
    len(DIL_PAIRS) * DIL_HEADS * HEAD_DIM, len(DIL_PAIRS) * DIL_HEADS * HEAD_DIM,
    len(DIL_PAIRS) * DIL_HEADS * HEAD_DIM,
    N_BRANCHES * D_MODEL,
)

VMEM_LIMIT_BYTES = 48 * 1024 * 1024


def _pick_tile(dim, pref):
    t = min(dim, pref)
    while dim % t:
        t //= 2
    return t


def _mm_kernel(a_ref, b_ref, o_ref):
    o_ref[...] = jnp.dot(a_ref[...], b_ref[...], preferred_element_type=jnp.float32).astype(o_ref.dtype)


def _matmul(a, b, *, tm=1024, tn=1024, out_dtype=jnp.float32):
    m, k = a.shape
    k2, n = b.shape
    assert k == k2
    tm = _pick_tile(m, tm)
    tn = _pick_tile(n, tn)
    return pl.pallas_call(
        _mm_kernel,
        grid=(n // tn, m // tm),
        in_specs=[pl.BlockSpec((tm, k), lambda j, i: (i, 0)),
                  pl.BlockSpec((k, tn), lambda j, i: (0, j))],
        out_specs=pl.BlockSpec((tm, tn), lambda j, i: (i, j)),
        out_shape=jax.ShapeDtypeStruct((m, n), out_dtype),
        compiler_params=pltpu.CompilerParams(
            dimension_semantics=("arbitrary", "arbitrary"), vmem_limit_bytes=VMEM_LIMIT_BYTES),
        name="matmul",
    )(a, b)


def _bmm_kernel(a_ref, b_ref, o_ref):
    o_ref[...] = jnp.dot(a_ref[...], b_ref[...], preferred_element_type=jnp.float32).astype(o_ref.dtype)


def _bmm(a, b, *, tm=1024, tn=1024, out_dtype=jnp.float32):
    e, m, k = a.shape
    _, k2, n = b.shape
    assert k == k2
    tm = _pick_tile(m, tm)
    tn = _pick_tile(n, tn)
    return pl.pallas_call(
        _bmm_kernel,
        grid=(e, n // tn, m // tm),
        in_specs=[pl.BlockSpec((None, tm, k), lambda g, j, i: (g, i, 0)),
                  pl.BlockSpec((None, k, tn), lambda g, j, i: (g, 0, j))],
        out_specs=pl.BlockSpec((None, tm, tn), lambda g, j, i: (g, i, j)),
        out_shape=jax.ShapeDtypeStruct((e, m, n), out_dtype),
        compiler_params=pltpu.CompilerParams(
            dimension_semantics=("arbitrary", "arbitrary", "arbitrary"), vmem_limit_bytes=VMEM_LIMIT_BYTES),
        name="expert_matmul",
    )(a, b)


def _rms_norm(x, g):
    xf = x.astype(jnp.float32)
    y = xf * lax.rsqrt(jnp.mean(xf * xf, axis=-1, keepdims=True) + EPS)
    return y * g.astype(jnp.float32)


def _flip(t):
    return jnp.flip(t, axis=1)


def _rope_tables(pos, dims, theta):
    freqs = theta ** (-jnp.arange(0, dims, 2, dtype=jnp.float32) / dims)
    ang = pos[:, None] * freqs[None, :]
    return jnp.cos(ang), jnp.sin(ang)


def _rotary(t, cos, sin):
    m = t.shape[-1] // 2
    c = cos[None, :, None, :]
    s = sin[None, :, None, :]
    t1, t2 = t[..., :m], t[..., m:]
    return jnp.concatenate([t1 * c - t2 * s, t2 * c + t1 * s], axis=-1)


def _blocked_gqa(q, k, v):
    b, L, hq, hd = q.shape
    hkv = k.shape[2]
    rep = hq // hkv
    qg = q.reshape(b, L, hkv, rep, hd)
    s = jnp.einsum('bqgrd,bkgd->bgrqk', qg, k).astype(jnp.float32)
    p = jax.nn.softmax(s, axis=-1)
    o = jnp.einsum('bgrqk,bkgd->bqgrd', p, v)
    return o.reshape(b, L, hq * hd)


def _axial_gqa_branch(aq, ak, av, q_norm, k_norm):
    b, L, _ = aq.shape
    rows = L // GRID_W
    row = jnp.repeat(jnp.arange(rows, dtype=jnp.float32), GRID_W)
    col = jnp.arange(L, dtype=jnp.float32) - row * GRID_W
    half = HEAD_DIM // 2
    cr, sr = _rope_tables(row, half, AXIAL_THETA)
    cc, sc = _rope_tables(col, half, AXIAL_THETA)

    def axial(t):
        return jnp.concatenate([_rotary(t[..., :half], cr, sr), _rotary(t[..., half:], cc, sc)], axis=-1)

    q = axial(_rms_norm(aq.reshape(b, L, ATTN_HEADS, HEAD_DIM), q_norm)) * HEAD_DIM ** -0.5
    k = axial(_rms_norm(ak.reshape(b, L, ATTN_KV_HEADS, HEAD_DIM), k_norm))
    v = av.reshape(b, L, ATTN_KV_HEADS, HEAD_DIM)
    return _blocked_gqa(q, k, v)


def _centred_depthwise_conv(x, w):
    kw = w.shape[0]
    return lax.conv_general_dilated(
        x, w[:, None, :], window_strides=(1,), padding=[(kw // 2, kw // 2)],
        dimension_numbers=('NWC', 'WIO', 'NWC'), feature_group_count=x.shape[-1])


def _ssd_scan(x, a, bm, cm):
    b, L, g, hg, p = x.shape
    n = bm.shape[-1]
    nc, l = L // SSM_CHUNK, SSM_CHUNK
    x = x.reshape(b, nc, l, g, hg, p)
    a = a.reshape(b, nc, l, g, hg)
    bm = bm.reshape(b, nc, l, g, n)
    cm = cm.reshape(b, nc, l, g, n)
    acs = jnp.cumsum(a, axis=2)
    lower = jnp.tril(jnp.ones((l, l), dtype=bool))
    seg = acs[:, :, :, None] - acs[:, :, None, :]
    decay = jnp.exp(jnp.where(lower[:, :, None, None], seg, -jnp.inf))
    scores = jnp.einsum('bclgn,bcsgn->bclsg', cm, bm)
    y = jnp.einsum('bclsgh,bcsghp->bclghp', scores[..., None] * decay, x)
    a_last = acs[:, :, -1]
    states = jnp.einsum('bclgn,bclghp->bcghpn', bm, x * jnp.exp(a_last[:, :, None] - acs)[..., None])

    def step(s, inp):
        dec, st = inp
        return jnp.exp(dec)[..., None, None] * s + st, s

    _, s_prev = lax.scan(step, jnp.zeros((b, g, hg, p, n), jnp.float32),
                         (jnp.moveaxis(a_last, 1, 0), jnp.moveaxis(states, 1, 0)))
    s_prev = jnp.moveaxis(s_prev, 0, 1)
    y = y + jnp.einsum('bclgn,bcghpn->bclghp', cm, s_prev) * jnp.exp(acs)[..., None]
    return y.reshape(b, L, g, hg, p)


def _ssd_branch(sz, sxbc, sdt, conv_w, conv_b, a_log, dt_bias, d_skip, norm_g):
    b, L, _ = sxbc.shape
    hg = SSM_HEADS // SSM_GROUPS
    xbc = jax.nn.silu(_centred_depthwise_conv(sxbc, conv_w) + conv_b)
    xs, bm, cm = jnp.split(xbc, [SSM_INNER, SSM_INNER + SSM_GROUPS * SSM_STATE], axis=-1)
    xs = xs.reshape(b, L, SSM_GROUPS, hg, SSM_HEAD_DIM)
    bm = bm.reshape(b, L, SSM_GROUPS, SSM_STATE)
    cm = cm.reshape(b, L, SSM_GROUPS, SSM_STATE)
    dt = jax.nn.softplus(sdt.reshape(b, L, 2, SSM_HEADS) + dt_bias)
    dt = dt.reshape(b, L, 2, SSM_GROUPS, hg)
    a = -jnp.exp(a_log).reshape(2, SSM_GROUPS, hg)
    y_f = _ssd_scan(xs * dt[:, :, 0, ..., None], dt[:, :, 0] * a[0], bm, cm)
    y_b = _flip(_ssd_scan(_flip(xs * dt[:, :, 1, ..., None]), _flip(dt[:, :, 1] * a[1]), _flip(bm), _flip(cm)))
    y = y_f + y_b + d_skip.reshape(SSM_GROUPS, hg, 1) * xs
    y = y.reshape(b, L, SSM_INNER) * jax.nn.silu(sz)
    y = _rms_norm(y.reshape(b, L, SSM_GROUPS, SSM_INNER // SSM_GROUPS), norm_g.reshape(SSM_GROUPS, -1))
    return y.reshape(b, L, SSM_INNER)


def _gla_scan(q, k, v, g):
    b, L, h, dk = q.shape
    dv = v.shape[-1]
    nc, l = L // GLA_CHUNK, GLA_CHUNK
    q = q.reshape(b, nc, l, h, dk)
    k = k.reshape(b, nc, l, h, dk)
    g = g.reshape(b, nc, l, h, dk)
    v = v.reshape(b, nc, l, h, dv)
    gc = jnp.cumsum(g, axis=2)
    g_last = gc[:, :, -1]
    q_e = q * jnp.exp(gc)
    k_e = k * jnp.exp(-gc)
    k_s = k * jnp.exp(g_last[:, :, None] - gc)
    lower = jnp.tril(jnp.ones((l, l), dtype=bool))
    att = jnp.where(lower, jnp.einsum('bclhk,bcshk->bchls', q_e, k_e), 0.0)
    o = jnp.einsum('bchls,bcshv->bclhv', att, v)
    chunk_states = jnp.einsum('bclhk,bclhv->bchkv', k_s, v)

    def step(s, inp):
        dec, st = inp
        return jnp.exp(dec)[..., None] * s + st, s

    _, s_prev = lax.scan(step, jnp.zeros((b, h, dk, dv), jnp.float32),
                         (jnp.moveaxis(g_last, 1, 0), jnp.moveaxis(chunk_states, 1, 0)))
    o = o + jnp.einsum('bclhk,bchkv->bclhv', q_e, jnp.moveaxis(s_prev, 0, 1))
    return o.reshape(b, L, h, dv)


def _gla_branch(gq, gk, gv, gr, glr, gate_w, gate_b, norm_g):
    b, L, _ = gq.shape
    q = gq.reshape(b, L, GLA_HEADS, GLA_KEY_DIM) * GLA_KEY_DIM ** -0.5
    k = gk.reshape(b, L, GLA_HEADS, GLA_KEY_DIM)
    v = gv.reshape(b, L, GLA_HEADS, GLA_VAL_DIM)
    lr = glr.reshape(b, L, 2, GLA_GATE_RANK)
    logdecay = jax.nn.log_sigmoid(jnp.einsum('bldr,drk->bldk', lr, gate_w) + gate_b) / GLA_GATE_NORMALIZER
    logdecay = logdecay.reshape(b, L, 2, GLA_HEADS, GLA_KEY_DIM)
    o_f = _gla_scan(q, k, v, logdecay[:, :, 0])
    o_b = _flip(_gla_scan(_flip(q), _flip(k), _flip(v), _flip(logdecay[:, :, 1])))
    o = _rms_norm(o_f + o_b, norm_g) * jax.nn.silu(gr.reshape(b, L, GLA_HEADS, GLA_VAL_DIM))
    return o.reshape(b, L, GLA_HEADS * GLA_VAL_DIM)


def _banded_attention(q, k, v, half):
    nbat, h, n, hd = q.shape
    nblk = -(-n // half)
    npad = nblk * half
    qb = jnp.pad(q, ((0, 0), (0, 0), (0, npad - n), (0, 0))).reshape(nbat, h, nblk, half, hd)
    pad_kv = ((0, 0), (0, 0), (half, npad - n + half), (0, 0))

    def bands(t):
        tp = jnp.pad(t, pad_kv).reshape(nbat, h, nblk + 2, half, hd)
        return jnp.concatenate([tp[:, :, :nblk], tp[:, :, 1:nblk + 1], tp[:, :, 2:]], axis=3)

    kb, vb = bands(k), bands(v)
    s = jnp.einsum('bhiqd,bhikd->bhiqk', qb, kb).astype(jnp.float32)
    start = jnp.arange(nblk)[:, None] * half
    qpos = start + jnp.arange(half)[None, :]
    kpos = start - half + jnp.arange(3 * half)[None, :]
    valid = ((jnp.abs(qpos[:, :, None] - kpos[:, None, :]) <= half)
             & (kpos >= 0)[:, None, :] & (kpos < n)[:, None, :])
    s = jnp.where(valid, s, -jnp.inf)
    lse = jax.nn.logsumexp(s, axis=-1)
    p = jnp.exp(s - lse[..., None])
    o = jnp.einsum('bhiqk,bhikd->bhiqd', p, vb)
    return o.reshape(nbat, h, npad, hd)[:, :, :n], lse.reshape(nbat, h, npad)[:, :, :n]


def _to_residues(t, dil):
    b, L, h, hd = t.shape
    return t.reshape(b, L // dil, dil, h, hd).transpose(0, 2, 3, 1, 4).reshape(b * dil, h, L // dil, hd)


def _dilated_branch(dq, dk, dv, q_norm, k_norm):
    b, L, _ = dq.shape
    shape = (b, L, len(DIL_PAIRS) * DIL_HEADS, HEAD_DIM)
    cos, sin = _rope_tables(jnp.arange(L, dtype=jnp.float32), ROPE_DIMS, ROPE_THETA)

    def partial_rope(t):
        return jnp.concatenate([_rotary(t[..., :ROPE_DIMS], cos, sin), t[..., ROPE_DIMS:]], axis=-1)

    q = partial_rope(_rms_norm(dq.reshape(shape), q_norm)) * HEAD_DIM ** -0.5
    k = partial_rope(_rms_norm(dk.reshape(shape), k_norm))
    v = dv.reshape(shape)
    outs, lses = [], []
    for gi, (window, dil) in enumerate(DIL_PAIRS):
        hs = slice(gi * DIL_HEADS, (gi + 1) * DIL_HEADS)
        o, lse = _banded_attention(_to_residues(q[:, :, hs], dil), _to_residues(k[:, :, hs], dil),
                                   _to_residues(v[:, :, hs], dil), window // (2 * dil))
        o = o.reshape(b, dil, DIL_HEADS, L // dil, HEAD_DIM).transpose(0, 3, 1, 2, 4).reshape(b, L, DIL_HEADS, HEAD_DIM)
        lse = lse.reshape(b, dil, DIL_HEADS, L // dil).transpose(0, 3, 1, 2).reshape(b, L, DIL_HEADS)
        outs.append(o)
        lses.append(lse)
    wts = jax.nn.softmax(jnp.stack(lses, axis=0), axis=0)
    o = wts[0][..., None] * outs[0]
    for gi in range(1, len(DIL_PAIRS)):
        o = o + wts[gi][..., None] * outs[gi]
    return o.reshape(b, L, DIL_HEADS * HEAD_DIM)


def _token_mixer(h, w_in, attn_q_norm, attn_k_norm, ssm_conv_w, ssm_conv_b, ssm_a_log, ssm_dt_bias,
                 ssm_d, ssm_norm, gla_gate_w, gla_gate_b, gla_norm, dil_q_norm, dil_k_norm, w_branch, w_out):
    b, L, d = h.shape
    bf16 = jnp.bfloat16
    offs = [0]
    for size in SPLIT_SIZES:
        offs.append(offs[-1] + size)
    w_bf = w_in.astype(bf16)
    w_small = jnp.concatenate([w_bf[:, offs[5]:offs[6]], w_bf[:, offs[10]:offs[11]],
                               jnp.zeros((d, 64), bf16)], axis=1)
    w_main = jnp.concatenate([w_bf[:, :offs[5]], w_bf[:, offs[6]:offs[10]], w_bf[:, offs[11]:]], axis=1)
    hb = h.reshape(b * L, d).astype(bf16)
    u = _matmul(hb, w_main).reshape(b, L, -1)
    us = _matmul(hb, w_small).reshape(b, L, -1)
    sdt, glr = us[..., :32], us[..., 32:64]
    sizes_main = [s for i, s in enumerate(SPLIT_SIZES) if i not in (5, 10)]
    points, acc = [], 0
    for size in sizes_main[:-1]:
        acc += size
        points.append(acc)
    (aq, ak, av, sz, sxbc, gq, gk, gv, gr, dq, dk, dv, gate_logits) = jnp.split(u, points, axis=-1)
    branches = (
        _axial_gqa_branch(aq, ak, av, attn_q_norm, attn_k_norm),
        _ssd_branch(sz, sxbc, sdt, ssm_conv_w, ssm_conv_b, ssm_a_log, ssm_dt_bias, ssm_d, ssm_norm),
        _gla_branch(gq, gk, gv, gr, glr, gla_gate_w, gla_gate_b, gla_norm),
        _dilated_branch(dq, dk, dv, dil_q_norm, dil_k_norm),
    )
    gates = jax.nn.sigmoid(gate_logits.reshape(b, L, N_BRANCHES, D_MODEL))
    merged = None
    for i in range(N_BRANCHES):
        pi = _matmul(branches[i].reshape(b * L, -1).astype(bf16), w_branch[i].astype(bf16)).reshape(b, L, d)
        term = gates[:, :, i] * pi
        merged = term if merged is None else merged + term
    return _matmul(merged.reshape(b * L, d).astype(bf16), w_out.astype(bf16)).reshape(b, L, d)


def _expert_choice_ffn(h, w_router, w_gate, w_up, w_down):
    b, L, d = h.shape
    n = b * L
    bf16 = jnp.bfloat16
    cap = EC_CAPACITY * n // N_EXPERTS
    t = h.reshape(n, d)
    aff = jax.nn.softmax(jnp.dot(t, w_router, precision=lax.Precision.HIGHEST), axis=-1)
    gate, idx = lax.top_k(aff.T, cap)
    xe = jnp.take(t.astype(bf16), idx, axis=0)
    hg = _bmm(xe, w_gate.astype(bf16))
    hu = _bmm(xe, w_up.astype(bf16))
    hid = (jax.nn.silu(hg) * hu).astype(bf16)
    ye = _bmm(hid, w_down.astype(bf16)) * gate[..., None]
    out = jnp.zeros_like(t).at[idx.reshape(-1)].add(ye.reshape(-1, d))
    return out.reshape(b, L, d)


def kernel(x_prompt, x_sample, norm_mix, w_in, attn_q_norm, attn_k_norm, ssm_conv_w, ssm_conv_b, ssm_a_log,
           ssm_dt_bias, ssm_d, ssm_norm, gla_gate_w, gla_gate_b, gla_norm, dil_q_norm, dil_k_norm, w_branch,
           w_out, norm_ffn, w_router, w_expert_gate, w_expert_up, w_expert_down):
    def trunk(x):
        for l in range(DEPTH):
            x = x + _token_mixer(_rms_norm(x, norm_mix[l]), w_in[l], attn_q_norm[l], attn_k_norm[l],
                                 ssm_conv_w[l], ssm_conv_b[l], ssm_a_log[l], ssm_dt_bias[l], ssm_d[l],
                                 ssm_norm[l], gla_gate_w[l], gla_gate_b[l], gla_norm[l], dil_q_norm[l],
                                 dil_k_norm[l], w_branch[l], w_out[l])
            x = x + _expert_choice_ffn(_rms_norm(x, norm_ffn[l]), w_router[l], w_expert_gate[l],
                                       w_expert_up[l], w_expert_down[l])
        return x

    return (trunk(x_prompt), trunk(x_sample))
```
